```python
import jax, jax.numpy as jnp
from jax import lax
import numpy as np

D_MODEL = 1024
BATCH = 16
SEQ = 2048
DEPTH = 1

N_MOD = 6
NORM_EPS = 1e-6
RWKV_HEAD_DIM = 64
RWKV_HEADS = D_MODEL // RWKV_HEAD_DIM
RWKV_WIDTH = RWKV_HEADS * RWKV_HEAD_DIM
DECAY_LORA = 64
A_LORA = 64
GATE_LORA = 128
RWKV_GN_EPS = 64e-5
RWKV_COLS = 3 * RWKV_WIDTH + DECAY_LORA + A_LORA + GATE_LORA
ATTN_GROUPS = ((128, 1), (512, 4), (2048, 16))
N_GROUPS = len(ATTN_GROUPS)
HEADS_PER_GROUP = 4
ATTN_HEAD_DIM = 64
ATTN_WIDTH = N_GROUPS * HEADS_PER_GROUP * ATTN_HEAD_DIM
ATTN_OUT_WIDTH = HEADS_PER_GROUP * ATTN_HEAD_DIM
ROPE_DIM = ATTN_HEAD_DIM // 4
ROPE_THETA = 500000.0
IN_SPLITS = (RWKV_COLS, ATTN_WIDTH, ATTN_WIDTH, ATTN_WIDTH, D_MODEL, D_MODEL)
IN_COLS = sum(IN_SPLITS)
PEER_HEADS = 8
PEER_N_KEYS = 128
PEER_N_EXPERTS = PEER_N_KEYS * PEER_N_KEYS
PEER_QUERY_DIM = 256
PEER_HALF = PEER_QUERY_DIM // 2
PEER_TOPK = 16
PEER_CHUNK = 128

kernel_name = 'hybrid_rwkv7_dilated_attn_peer_block'


def _split(z, sizes):
    offs = np.cumsum(sizes)[:-1].tolist()
    return jnp.split(z, offs, axis=-1)


def rmsnorm(x, g):
    xf = x.astype(jnp.float32)
    y = xf * lax.rsqrt(jnp.mean(xf * xf, axis=-1, keepdims=True) + NORM_EPS)
    return (y * g.astype(jnp.float32)).astype(x.dtype)


def modulate(n, shift, scale):
    return n * (1 + scale[:, None, :]) + shift[:, None, :]


def partial_rope(x, pos):
    half = ROPE_DIM // 2
    inv = ROPE_THETA ** (-jnp.arange(half, dtype=jnp.float32) / half)
    ang = pos.astype(jnp.float32)[:, None] * inv[None, :]
    bshape = (pos.shape[0],) + (1,) * (x.ndim - 3) + (half,)
    cos, sin = jnp.cos(ang).reshape(bshape), jnp.sin(ang).reshape(bshape)
    xf = x.astype(jnp.float32)
    x1, x2, xp = xf[..., :half], xf[..., half:ROPE_DIM], xf[..., ROPE_DIM:]
    out = jnp.concatenate([x1 * cos - x2 * sin, x2 * cos + x1 * sin, xp], axis=-1)
    return out.astype(x.dtype)


def wkv7_scan(r, w, k, v, a, b):
    B, S, H, N = r.shape
    xs = tuple(t.astype(jnp.float32).transpose(1, 0, 2, 3) for t in (r, w, k, v, a, b))

    def step(state, inp):
        r_t, w_t, k_t, v_t, a_t, b_t = inp
        sa = jnp.einsum('bhij,bhj->bhi', state, a_t)
        state = (state * w_t[:, :, None, :] + sa[..., None] * b_t[:, :, None, :]
                 + v_t[..., None] * k_t[:, :, None, :])
        return state, jnp.einsum('bhij,bhj->bhi', state, r_t)

    init = jnp.zeros((B, H, N, N), jnp.float32)
    _, ys = lax.scan(step, init, xs)
    return ys.transpose(1, 0, 2, 3)


def rwkv7_time_mix(z, rwkv_mu, w0, w2, a0, a2, g2, k_k, k_a, r_k, lnx_g, lnx_b):
    B, S, _ = z.shape
    H, N = RWKV_HEADS, RWKV_HEAD_DIM
    z_prev = jnp.pad(z, ((0, 0), (1, 0), (0, 0)))[:, :-1]
    z = z + (z_prev - z) * rwkv_mu
    r, k, v, wl, al, gl = _split(z, (RWKV_WIDTH, RWKV_WIDTH, RWKV_WIDTH, DECAY_LORA, A_LORA, GATE_LORA))
    w = -jax.nn.softplus(-(w0 + jnp.tanh(wl) @ w2)) - 0.5
    decay = jnp.exp(-jnp.exp(w.astype(jnp.float32)))
    a = jax.nn.sigmoid(a0 + al @ a2)
    g = jax.nn.sigmoid(gl) @ g2
    hv = lambda t: t.reshape(B, S, H, N)
    kk = hv(k * k_k).astype(jnp.float32)
    kk = kk / jnp.maximum(jnp.linalg.norm(kk, axis=-1, keepdims=True), 1e-12)
    k = k * (1 + (a - 1) * k_a)
    r_h, k_h, v_h, a_h = hv(r), hv(k), hv(v), hv(a)
    y = wkv7_scan(r_h, hv(decay), k_h, v_h, -kk, kk * a_h)
    mu = jnp.mean(y, axis=-1, keepdims=True)
    var = jnp.mean(jnp.square(y - mu), axis=-1, keepdims=True)
    y = (y - mu) * lax.rsqrt(var + RWKV_GN_EPS) * lnx_g.reshape(H, N) + lnx_b.reshape(H, N)
    y = y + jnp.sum(r_h * k_h * r_k, axis=-1, keepdims=True) * v_h
    return y.reshape(B, S, RWKV_WIDTH).astype(z.dtype) * g


def dilated_group_attention(q, k, v, window, dilation):
    B, S, H, hd = q.shape
    L = S // dilation
    blk = window // dilation
    nb = -(-L // blk)
    Lp = nb * blk

    def to_sub(t):
        t = t.reshape(B, L, dilation, H, hd).transpose(0, 2, 3, 1, 4)
        t = jnp.pad(t, ((0, 0), (0, 0), (0, 0), (0, Lp - L), (0, 0)))
        return t.reshape(B, dilation, H, nb, blk, hd).astype(jnp.float32)

    qb, kb, vb = to_sub(q), to_sub(k), to_sub(v)

    def band(t):
        prev = jnp.pad(t, ((0, 0), (0, 0), (0, 0), (1, 0), (0, 0), (0, 0)))[:, :, :, :-1]
        return jnp.concatenate([prev, t], axis=-2)

    kk, vv = band(kb), band(vb)
    s = jnp.einsum('bdhnqe,bdhnke->bdhnqk', qb, kk) * (hd ** -0.5)
    qi = jnp.arange(blk)[:, None]
    kj = jnp.arange(2 * blk)[None, :]
    dist = blk + qi - kj
    band_mask = (dist >= 0) & (dist <= blk)
    mask = band_mask[None] & ((jnp.arange(nb)[:, None, None] > 0) | (kj[None] >= blk))
    s = jnp.where(mask, s, -jnp.inf)
    lse = jax.nn.logsumexp(s, axis=-1)
    p = jnp.exp(s - lse[..., None])
    o = jnp.einsum('bdhnqk,bdhnke->bdhnqe', p, vv)
    o = o.reshape(B, dilation, H, Lp, hd)[:, :, :, :L].transpose(0, 3, 1, 2, 4).reshape(B, S, H, hd)
    lse = lse.reshape(B, dilation, H, Lp)[..., :L].transpose(0, 3, 1, 2).reshape(B, S, H)
    return o, lse


def dilated_attention(zq, zk, zv, q_norm_g, k_norm_g, pos):
    B, S, _ = zq.shape
    shp = (B, S, N_GROUPS, HEADS_PER_GROUP, ATTN_HEAD_DIM)
    q = partial_rope(rmsnorm(zq.reshape(shp), q_norm_g[:, None, :]), pos)
    k = partial_rope(rmsnorm(zk.reshape(shp), k_norm_g[:, None, :]), pos)
    v = zv.reshape(shp)
    outs, lses = [], []
    for gi, (win, dil) in enumerate(ATTN_GROUPS):
        o, l = dilated_group_attention(q[:, :, gi], k[:, :, gi], v[:, :, gi], win, dil)
        outs.append(o)
        lses.append(l)
    wts = jax.nn.softmax(jnp.stack(lses), axis=0)
    y = jnp.sum(wts[..., None] * jnp.stack(outs), axis=0)
    return y.reshape(B, S, ATTN_OUT_WIDTH).astype(zq.dtype)


def peer(u, peer_wq, peer_k1, peer_k2, peer_u, peer_v):
    B, S, D = u.shape
    q = (u @ peer_wq).reshape(B, S, PEER_HEADS, PEER_QUERY_DIM).astype(jnp.float32)
    s1 = jnp.einsum('bshe,ne->bshn', q[..., :PEER_HALF], peer_k1.astype(jnp.float32))
    s2 = jnp.einsum('bshe,ne->bshn', q[..., PEER_HALF:], peer_k2.astype(jnp.float32))
    v1, i1 = lax.top_k(s1, PEER_TOPK)
    v2, i2 = lax.top_k(s2, PEER_TOPK)
    cshape = (B, S, PEER_HEADS, PEER_TOPK * PEER_TOPK)
    cand = (v1[..., :, None] + v2[..., None, :]).reshape(cshape)
    cidx = (i1[..., :, None] * PEER_N_KEYS + i2[..., None, :]).reshape(cshape)
    top, sel = lax.top_k(cand, PEER_TOPK)
    idx = jnp.take_along_axis(cidx, sel, axis=-1)
    gate = jax.nn.softmax(top, axis=-1)
    n_chunks = (B * S) // PEER_CHUNK
    hk = PEER_HEADS * PEER_TOPK
    u_c = u.reshape(n_chunks, PEER_CHUNK, D)
    i_c = idx.reshape(n_chunks, PEER_CHUNK, hk)
    g_c = gate.reshape(n_chunks, PEER_CHUNK, hk)

    def chunk(args):
        uc, ic, gc = args
        ue = jnp.take(peer_u, ic, axis=0)
        act = jax.nn.gelu(jnp.einsum('cd,ced->ce', uc, ue).astype(jnp.float32), approximate=False)
        ve = jnp.take(peer_v, ic, axis=0)
        return jnp.einsum('ce,ced->cd', (gc * act).astype(ve.dtype), ve)

    out = lax.map(chunk, (u_c, i_c, g_c))
    return out.reshape(B, S, D).astype(u.dtype)


def _layer(x, c, pos, w_ada, b_ada, norm1_g, w_in, rwkv_mu, w0, w2, a0, a2, g2, k_k, k_a, r_k,
           lnx_g, lnx_b, q_norm_g, k_norm_g, w_br_rwkv, w_br_attn, w_out, norm2_g,
           peer_wq, peer_k1, peer_k2, peer_u, peer_v):
    mod = jax.nn.silu(c) @ w_ada + b_ada
    sh1, sc1, gt1, sh2, sc2, gt2 = jnp.split(mod, N_MOD, axis=-1)
    n1 = modulate(rmsnorm(x, norm1_g), sh1, sc1)
    z = n1 @ w_in
    z_rwkv, zq, zk, zv, z_gr, z_ga = _split(z, IN_SPLITS)
    y_r = rwkv7_time_mix(z_rwkv, rwkv_mu, w0, w2, a0, a2, g2, k_k, k_a, r_k, lnx_g, lnx_b)
    y_a = dilated_attention(zq, zk, zv, q_norm_g, k_norm_g, pos)
    merged = jax.nn.sigmoid(z_gr) * (y_r @ w_br_rwkv) + jax.nn.sigmoid(z_ga) * (y_a @ w_br_attn)
    h = x + gt1[:, None, :] * (merged @ w_out)
    n2 = modulate(rmsnorm(h, norm2_g), sh2, sc2)
    return h + gt2[:, None, :] * peer(n2, peer_wq, peer_k1, peer_k2, peer_u, peer_v)


def setup_inputs(seed: int = 0) -> dict:
    key = jax.random.key(seed)
    ks = iter(jax.random.split(key, 32))
    f32 = jnp.float32
    D, L = D_MODEL, DEPTH

    def nrm(shape, scale):
        return jax.random.normal(next(ks), shape, f32) * scale

    return {
        'x': nrm((BATCH, SEQ, D), 1.0),
        'c': nrm((BATCH, D), 1.0),
        'w_ada': nrm((L, D, N_MOD * D), 0.2 * D ** -0.5),
        'b_ada': nrm((L, N_MOD * D), 0.02),
        'norm1_g': 1.0 + nrm((L, D), 0.05),
        'w_in': nrm((L, D, IN_COLS), D ** -0.5),
        'rwkv_mu': jax.random.uniform(next(ks), (L, RWKV_COLS), f32, 0.05, 0.95),
        'w0': nrm((L, RWKV_WIDTH), 0.5) - 1.0,
        'w2': nrm((L, DECAY_LORA, RWKV_WIDTH), 0.5 * DECAY_LORA ** -0.5),
        'a0': nrm((L, RWKV_WIDTH), 0.3),
        'a2': nrm((L, A_LORA, RWKV_WIDTH), A_LORA ** -0.5),
        'g2': nrm((L, GATE_LORA, RWKV_WIDTH), GATE_LORA ** -0.5),
        'k_k': 0.85 + nrm((L, RWKV_WIDTH), 0.05),
        'k_a': 1.0 + nrm((L, RWKV_WIDTH), 0.05),
        'r_k': nrm((L, RWKV_HEADS, RWKV_HEAD_DIM), 0.1),
        'lnx_g': 1.0 + nrm((L, RWKV_WIDTH), 0.05),
        'lnx_b': nrm((L, RWKV_WIDTH), 0.02),
        'q_norm_g': 1.0 + nrm((L, N_GROUPS, ATTN_HEAD_DIM), 0.05),
        'k_norm_g': 1.0 + nrm((L, N_GROUPS, ATTN_HEAD_DIM), 0.05),
        'w_br_rwkv': nrm((L, RWKV_WIDTH, D), RWKV_WIDTH ** -0.5),
        'w_br_attn': nrm((L, ATTN_OUT_WIDTH, D), ATTN_OUT_WIDTH ** -0.5),
        'w_out': nrm((L, D, D), D ** -0.5),
        'norm2_g': 1.0 + nrm((L, D), 0.05),
        'peer_wq': nrm((L, D, PEER_HEADS * PEER_QUERY_DIM), D ** -0.5),
        'peer_k1': nrm((L, PEER_N_KEYS, PEER_HALF), PEER_HALF ** -0.5),
        'peer_k2': nrm((L, PEER_N_KEYS, PEER_HALF), PEER_HALF ** -0.5),
        'peer_u': nrm((L, PEER_N_EXPERTS, D), D ** -0.5),
        'peer_v': nrm((L, PEER_N_EXPERTS, D), PEER_HEADS ** -0.5),
    }


def reference(x, c, w_ada, b_ada, norm1_g, w_in, rwkv_mu, w0, w2, a0, a2, g2, k_k, k_a, r_k,
              lnx_g, lnx_b, q_norm_g, k_norm_g, w_br_rwkv, w_br_attn, w_out, norm2_g,
              peer_wq, peer_k1, peer_k2, peer_u, peer_v):
    pos = jnp.arange(x.shape[1], dtype=jnp.int32)
    h = x
    for l in range(DEPTH):
        h = _layer(h, c, pos, w_ada[l], b_ada[l], norm1_g[l], w_in[l], rwkv_mu[l], w0[l], w2[l],
                   a0[l], a2[l], g2[l], k_k[l], k_a[l], r_k[l], lnx_g[l], lnx_b[l],
                   q_norm_g[l], k_norm_g[l], w_br_rwkv[l], w_br_attn[l], w_out[l], norm2_g[l],
                   peer_wq[l], peer_k1[l], peer_k2[l], peer_u[l], peer_v[l])
    return h.astype(x.dtype)
```

```python
import functools

import jax
import jax.numpy as jnp
import numpy as np
from jax import lax
from jax.experimental import pallas as pl
from jax.experimental.pallas import tpu as pltpu

F32 = jnp.float32
BF16 = jnp.bfloat16

D_MODEL = 1024
N_MOD = 6
NORM_EPS = 1e-6
HEAD_DIM = 64
RWKV_HEADS = D_MODEL // HEAD_DIM
RWKV_WIDTH = D_MODEL
DECAY_LORA = 64
A_LORA = 64
GATE_LORA = 128
RWKV_GN_EPS = 64e-5
RWKV_COLS = 3 * RWKV_WIDTH + DECAY_LORA + A_LORA + GATE_LORA
ATTN_GROUPS = ((128, 1), (512, 4), (2048, 16))
N_GROUPS = len(ATTN_GROUPS)
HEADS_PER_GROUP = 4
ATTN_WIDTH = N_GROUPS * HEADS_PER_GROUP * HEAD_DIM
ATTN_OUT_WIDTH = HEADS_PER_GROUP * HEAD_DIM
ATTN_BLK = 128
ROPE_DIM = HEAD_DIM // 4
ROPE_THETA = 500000.0
GATE_COLS = 2 * D_MODEL
PEER_HEADS = 8
PEER_N_KEYS = 128
PEER_QUERY_DIM = 256
PEER_HALF = PEER_QUERY_DIM // 2
PEER_TOPK = 16
PEER_HK = PEER_HEADS * PEER_TOPK

LANES = 128
VMEM_LIMIT = 56 * 1024 * 1024

INPROJ_ROWS = 256
INPROJ_COL_CHUNK = 256
PREP_ROWS = 256
WKV_TIME_BLOCK = 32
WKV_ROW_BLOCK = 4
MERGE_ROWS = 256
TOPK_TOKENS = 256
PEER_TOKENS = 16


def _params(*sem):
    return pltpu.CompilerParams(dimension_semantics=sem, vmem_limit_bytes=VMEM_LIMIT)


def _ada_kernel(c_ref, w_ref, b_ref, o_ref):
    c = c_ref[...]
    o_ref[...] = jnp.dot(c * jax.nn.sigmoid(c), w_ref[...], preferred_element_type=F32,
                         precision=lax.Precision.HIGHEST) + b_ref[...]


def _ada(c, w_ada, b_ada):
    B = c.shape[0]
    return pl.pallas_call(
        _ada_kernel,
        grid=(N_MOD,),
        in_specs=[pl.BlockSpec((B, D_MODEL), lambda j: (0, 0)),
                  pl.BlockSpec((D_MODEL, D_MODEL), lambda j: (0, j)),
                  pl.BlockSpec((1, D_MODEL), lambda j: (0, j))],
        out_specs=pl.BlockSpec((B, D_MODEL), lambda j: (0, j)),
        out_shape=jax.ShapeDtypeStruct((B, N_MOD * D_MODEL), F32),
        compiler_params=_params("arbitrary"),
        name="ada",
    )(c, w_ada, b_ada.reshape(1, -1))


def _mod_spec(which, rows_per_batch_block):
    return pl.BlockSpec((1, 1, D_MODEL), lambda i: ((i // rows_per_batch_block) * N_MOD + which, 0, 0))


def _rms_mod(x, gain, shift, scale):
    ms = jnp.mean(x * x, axis=-1, keepdims=True)
    n = x * lax.rsqrt(ms + NORM_EPS) * gain
    return n * (1.0 + scale) + shift


def _inproj_kernel(x_ref, sh_ref, sc_ref, g_ref, w_ref, zr_ref, zqkv_ref, zg_ref):
    nb = _rms_mod(x_ref[...], g_ref[...], sh_ref[0], sc_ref[0]).astype(BF16)
    col = 0
    for ref in (zr_ref, zqkv_ref, zg_ref):
        width = ref.shape[1]
        for c0 in range(0, width, INPROJ_COL_CHUNK):
            ref[:, c0:c0 + INPROJ_COL_CHUNK] = jnp.dot(
                nb, w_ref[:, col + c0:col + c0 + INPROJ_COL_CHUNK], preferred_element_type=F32)
        col += width


def _inproj(x2, modr, norm1_g, w_in_b, S):
    T = x2.shape[0]
    tm = INPROJ_ROWS
    in_cols = w_in_b.shape[1]
    widths = (RWKV_COLS, 3 * ATTN_WIDTH, GATE_COLS)
    return pl.pallas_call(
        _inproj_kernel,
        grid=(T // tm,),
        in_specs=[pl.BlockSpec((tm, D_MODEL), lambda i: (i, 0)),
                  _mod_spec(0, S // tm), _mod_spec(1, S // tm),
                  pl.BlockSpec((1, D_MODEL), lambda i: (0, 0)),
                  pl.BlockSpec((D_MODEL, in_cols), lambda i: (0, 0), pipeline_mode=pl.Buffered(1))],
        out_specs=[pl.BlockSpec((tm, w), lambda i: (i, 0)) for w in widths],
        out_shape=[jax.ShapeDtypeStruct((T, w), F32) for w in widths],
        compiler_params=_params("arbitrary"),
        name="inproj",
    )(x2, modr, modr, norm1_g.reshape(1, -1), w_in_b)


def _segsum64(x, bd):
    outs = []
    for g in range(x.shape[1] // LANES):
        xs = x[:, g * LANES:(g + 1) * LANES]
        hi = xs.astype(BF16)
        lo = (xs - hi.astype(F32)).astype(BF16)
        outs.append(jnp.dot(hi, bd, preferred_element_type=F32) + jnp.dot(lo, bd, preferred_element_type=F32))
    return jnp.concatenate(outs, axis=1)


def _prep_kernel(z_ref, mu_ref, w0_ref, a0_ref, kk_ref, ka_ref, rk_ref, w2p_ref, a2p_ref, g2_ref, bd_ref,
                 r_ref, w_ref, k_ref, v_ref, a_ref, b_ref, g_ref, bonus_ref, carry_ref):
    ts = z_ref.shape[0]
    W = RWKV_WIDTH

    @pl.when(pl.program_id(1) == 0)
    def _():
        carry_ref[...] = jnp.zeros_like(carry_ref)

    z = z_ref[...]
    prev_last = carry_ref[0:1, :]
    rows = lax.broadcasted_iota(jnp.int32, z.shape, 0)
    zp = jnp.where(rows == 0, prev_last, pltpu.roll(z, 1, 0))
    carry_ref[0:1, :] = z[ts - 1:ts, :]
    zm = z + (zp - z) * mu_ref[...]

    r = zm[:, 0:W]
    k = zm[:, W:2 * W]
    v = zm[:, 2 * W:3 * W]
    lo1 = zm[:, 3 * W:3 * W + LANES]
    gl = zm[:, 3 * W + LANES:3 * W + 2 * LANES]
    bd = bd_ref[...]

    dw = jnp.dot(jnp.tanh(lo1).astype(BF16), w2p_ref[...], preferred_element_type=F32)
    nx = -(w0_ref[...] + dw)
    softplus = jnp.maximum(nx, 0.0) + jnp.log1p(jnp.exp(-jnp.abs(nx)))
    decay = jnp.exp(-jnp.exp(-softplus - 0.5))
    a = jax.nn.sigmoid(a0_ref[...] + jnp.dot(lo1.astype(BF16), a2p_ref[...], preferred_element_type=F32))
    g = jnp.dot(jax.nn.sigmoid(gl).astype(BF16), g2_ref[...], preferred_element_type=F32)

    kk = k * kk_ref[...]
    kk = kk / jnp.maximum(jnp.sqrt(_segsum64(kk * kk, bd)), 1e-12)
    k2 = k * (1.0 + (a - 1.0) * ka_ref[...])

    r_ref[...] = r
    w_ref[...] = decay
    k_ref[...] = k2
    v_ref[...] = v
    a_ref[...] = -kk
    b_ref[...] = kk * a
    g_ref[...] = g
    bonus_ref[...] = _segsum64(r * k2 * rk_ref[...], bd) * v


def _prep(z_r, B, S, mu, w0, a0, k_k, k_a, r_k, w2p, a2p, g2b, bd):
    T = z_r.shape[0]
    ts = PREP_ROWS
    nsb = S // ts
    row = lambda n: pl.BlockSpec((1, n), lambda b, s: (0, 0))
    full = lambda a: pl.BlockSpec(a.shape, lambda b, s: (0, 0))
    tile = pl.BlockSpec((ts, RWKV_WIDTH), lambda b, s: (b * nsb + s, 0))
    return pl.pallas_call(
        _prep_kernel,
        grid=(B, nsb),
        in_specs=[pl.BlockSpec((ts, RWKV_COLS), lambda b, s: (b * nsb + s, 0)),
                  row(RWKV_COLS), row(RWKV_WIDTH), row(RWKV_WIDTH), row(RWKV_WIDTH), row(RWKV_WIDTH), row(RWKV_WIDTH),
                  full(w2p), full(a2p), full(g2b), full(bd)],
        out_specs=[tile] * 8,
        out_shape=[jax.ShapeDtypeStruct((T, RWKV_WIDTH), F32)] * 8,
        scratch_shapes=[pltpu.VMEM((8, RWKV_COLS), F32)],
        compiler_params=_params("arbitrary", "arbitrary"),
        name="prep",
    )(z_r, mu.reshape(1, -1), w0.reshape(1, -1), a0.reshape(1, -1), k_k.reshape(1, -1), k_a.reshape(1, -1),
      r_k.reshape(1, -1), w2p, a2p, g2b, bd)


def _wkv_kernel(r_ref, w_ref, k_ref, v_ref, a_ref, b_ref, y_ref, st_ref):
    @pl.when(pl.program_id(1) == 0)
    def _():
        st_ref[...] = jnp.zeros_like(st_ref)

    n_t = r_ref.shape[0]
    rb = WKV_ROW_BLOCK

    def row_block(ib, carry):
        i0 = ib * rb
        state = tuple(st_ref[i0 + q] for q in range(rb))

        def step(t, state):
            a = a_ref[t]
            w = w_ref[t]
            b = b_ref[t]
            k = k_ref[t]
            r = r_ref[t]
            new_state, ys = [], []
            for q in range(rb):
                sq = state[q]
                sa = jnp.sum(sq * a, axis=0, keepdims=True)
                vq = v_ref[t, pl.ds(i0 + q, 1), :]
                sn = sq * w + sa * b + vq * k
                ys.append(jnp.sum(sn * r, axis=0, keepdims=True))
                new_state.append(sn)
            y_ref[t, pl.ds(i0, rb), :] = jnp.concatenate(ys, axis=0)
            return tuple(new_state)

        state = lax.fori_loop(0, n_t, step, state)
        for q in range(rb):
            st_ref[i0 + q] = state[q]
        return carry

    lax.fori_loop(0, HEAD_DIM // rb, row_block, 0)


def _wkv(r, w, k, v, a, b):
    S, N, BH = r.shape
    lb = min(LANES, BH)
    tsb = WKV_TIME_BLOCK
    spec = pl.BlockSpec((tsb, N, lb), lambda g, t: (t, 0, g))
    return pl.pallas_call(
        _wkv_kernel,
        grid=(BH // lb, S // tsb),
        in_specs=[spec] * 6,
        out_specs=spec,
        out_shape=jax.ShapeDtypeStruct((S, N, BH), F32),
        scratch_shapes=[pltpu.VMEM((N, N, lb), F32)],
        compiler_params=_params("arbitrary", "arbitrary"),
        name="wkv",
    )(r, w, k, v, a, b)


def _attn_kernel(*refs):
    q_refs, k_refs, v_refs = refs[0:3], refs[3:6], refs[6:9]
    cos_ref, s1_ref, s2_ref, qg_ref, kg_ref, o_ref = refs[9:15]
    qs, ks = refs[15:17]
    o_scr, m_scr, l_scr = refs[17:20], refs[20:23], refs[23:26]
    S = qs.shape[0]
    blk = ATTN_BLK
    lane_s = lax.broadcasted_iota(jnp.int32, (S, LANES), 1) < HEAD_DIM
    lane_b = lax.broadcasted_iota(jnp.int32, (blk, LANES), 1) < HEAD_DIM
    qi = lax.broadcasted_iota(jnp.int32, (blk, 2 * blk), 0)
    kj = lax.broadcasted_iota(jnp.int32, (blk, 2 * blk), 1)
    dist = blk + qi - kj
    band = (dist >= 0) & (dist <= blk)

    def norm_rope(x, gain):
        x2 = x * x
        ss0 = jnp.sum(jnp.where(lane_s, x2, 0.0), axis=-1, keepdims=True)
        ss1 = jnp.sum(jnp.where(lane_s, 0.0, x2), axis=-1, keepdims=True)
        ms = jnp.where(lane_s, ss0, ss1) * (1.0 / HEAD_DIM)
        xn = x * lax.rsqrt(ms + NORM_EPS) * gain
        return (xn * cos_ref[...] + pltpu.roll(xn, LANES - ROPE_DIM // 2, 1) * s1_ref[...]
                + pltpu.roll(xn, ROPE_DIM // 2, 1) * s2_ref[...])

    for g, (win, dil) in enumerate(ATTN_GROUPS):
        nb = (S // dil) // blk
        qs[...] = norm_rope(q_refs[g][...], qg_ref[g]) * (HEAD_DIM ** -0.5)
        ks[...] = norm_rope(k_refs[g][...], kg_ref[g])
        v_ref = v_refs[g]
        o_g, m_g, l_g = o_scr[g], m_scr[g], l_scr[g]

        def block(i, carry, dil=dil, nb=nb, v_ref=v_ref, o_g=o_g, m_g=m_g, l_g=l_g):
            res = i // nb
            n = i % nb
            qstart = res + dil * blk * n
            pstart = jnp.where(n > 0, qstart - dil * blk, qstart)
            cur = pl.ds(qstart, blk, stride=dil)
            prev = pl.ds(pstart, blk, stride=dil)
            qb = qs[cur, :]
            kcat = jnp.concatenate([ks[prev, :], ks[cur, :]], axis=0).astype(BF16)
            vcat = jnp.concatenate([v_ref[prev, :], v_ref[cur, :]], axis=0).astype(BF16)
            valid = band & ((n > 0) | (kj >= blk))
            per_head = []
            for hh in range(2):
                hm = lane_b if hh == 0 else jnp.logical_not(lane_b)
                qh = jnp.where(hm, qb, 0.0).astype(BF16)
                s = lax.dot_general(qh, kcat, (((1,), (1,)), ((), ())), preferred_element_type=F32)
                s = jnp.where(valid, s, -1e30)
                m = jnp.max(s, axis=-1, keepdims=True)
                p = jnp.exp(s - m)
                l = jnp.sum(p, axis=-1, keepdims=True)
                o = jnp.dot(p.astype(BF16), vcat, preferred_element_type=F32)
                per_head.append((o, m, l))
            (o0, m0, l0), (o1, m1, l1) = per_head
            o_g[cur, :] = jnp.where(lane_b, o0, o1)
            m_g[cur, :] = jnp.where(lane_b, m0, m1)
            l_g[cur, :] = jnp.where(lane_b, l0, l1)
            return carry

        lax.fori_loop(0, S // blk, block, 0)

    mx = jnp.maximum(jnp.maximum(m_scr[0][...], m_scr[1][...]), m_scr[2][...])
    num = jnp.zeros((S, LANES), F32)
    den = jnp.zeros((S, LANES), F32)
    for g in range(N_GROUPS):
        wg = jnp.exp(m_scr[g][...] - mx)
        num = num + wg * o_scr[g][...]
        den = den + wg * l_scr[g][...]
    o_ref[...] = num / den


def _rope_tables(S):
    half = ROPE_DIM // 2
    inv = ROPE_THETA ** (-jnp.arange(half, dtype=F32) / half)
    ang = jnp.arange(S, dtype=F32)[:, None] * inv[None, :]
    cos, sin = jnp.cos(ang), jnp.sin(ang)
    pad = jnp.zeros((S, HEAD_DIM - ROPE_DIM), F32)
    zeros = jnp.zeros((S, half), F32)
    c64 = jnp.concatenate([cos, cos, pad + 1.0], axis=1)
    s1_64 = jnp.concatenate([-sin, zeros, pad], axis=1)
    s2_64 = jnp.concatenate([zeros, sin, pad], axis=1)
    two = lambda t: jnp.concatenate([t, t], axis=1)
    return two(c64), two(s1_64), two(s2_64)


def _attn(z_qkv, B, S, q_norm_g, k_norm_g):
    T = z_qkv.shape[0]
    cos, s1, s2 = _rope_tables(S)
    qg = jnp.concatenate([q_norm_g, q_norm_g], axis=-1).reshape(N_GROUPS, 1, LANES)
    kg = jnp.concatenate([k_norm_g, k_norm_g], axis=-1).reshape(N_GROUPS, 1, LANES)
    pair_blocks = ATTN_WIDTH // LANES

    def col_spec(which, g):
        return pl.BlockSpec((S, LANES), lambda b, jp: (b, which * pair_blocks + g * 2 + jp))

    in_specs = [col_spec(w, g) for w in range(3) for g in range(N_GROUPS)]
    in_specs += [pl.BlockSpec((S, LANES), lambda b, jp: (0, 0))] * 3
    in_specs += [pl.BlockSpec((N_GROUPS, 1, LANES), lambda b, jp: (0, 0, 0))] * 2
    return pl.pallas_call(
        _attn_kernel,
        grid=(B, 2),
        in_specs=in_specs,
        out_specs=pl.BlockSpec((S, LANES), lambda b, jp: (b, jp)),
        out_shape=jax.ShapeDtypeStruct((T, ATTN_OUT_WIDTH), F32),
        scratch_shapes=[pltpu.VMEM((S, LANES), F32)] * (2 + 3 * N_GROUPS),
        compiler_params=_params("arbitrary", "arbitrary"),
        name="attn",
    )(*([z_qkv] * 9), cos, s1, s2, qg, kg)


def _merge_kernel(y_ref, bonus_ref, g_ref, ya_ref, zg_ref, x_ref, gt1_ref, sh2_ref, sc2_ref,
                  lg_ref, lb_ref, n2g_ref, wbr_ref, wba_ref, wo_ref, bd_ref, h_ref, n2_ref):
    bd = bd_ref[...]
    y = y_ref[...]
    mu = _segsum64(y, bd) * (1.0 / HEAD_DIM)
    yc = y - mu
    var = _segsum64(yc * yc, bd) * (1.0 / HEAD_DIM)
    yn = yc * lax.rsqrt(var + RWKV_GN_EPS) * lg_ref[...] + lb_ref[...]
    yr = (yn + bonus_ref[...]) * g_ref[...]
    br = jnp.dot(yr.astype(BF16), wbr_ref[...], preferred_element_type=F32)
    ba = jnp.dot(ya_ref[...].astype(BF16), wba_ref[...], preferred_element_type=F32)
    zg = zg_ref[...]
    merged = jax.nn.sigmoid(zg[:, :D_MODEL]) * br + jax.nn.sigmoid(zg[:, D_MODEL:]) * ba
    h = x_ref[...] + gt1_ref[0] * jnp.dot(merged.astype(BF16), wo_ref[...], preferred_element_type=F32)
    h_ref[...] = h
    n2_ref[...] = _rms_mod(h, n2g_ref[...], sh2_ref[0], sc2_ref[0])


def _merge(y, bonus, g, ya, z_g, x2, modr, lnx_g, lnx_b, norm2_g, wbr, wba, wo, bd, S):
    T = x2.shape[0]
    tm = MERGE_ROWS
    tile = lambda n: pl.BlockSpec((tm, n), lambda i: (i, 0))
    row = pl.BlockSpec((1, D_MODEL), lambda i: (0, 0))
    full = lambda a: pl.BlockSpec(a.shape, lambda i: (0, 0))
    return pl.pallas_call(
        _merge_kernel,
        grid=(T // tm,),
        in_specs=[tile(D_MODEL), tile(D_MODEL), tile(D_MODEL), tile(ATTN_OUT_WIDTH), tile(GATE_COLS), tile(D_MODEL),
                  _mod_spec(2, S // tm), _mod_spec(3, S // tm), _mod_spec(4, S // tm),
                  row, row, row, full(wbr), full(wba), full(wo), full(bd)],
        out_specs=[tile(D_MODEL), tile(D_MODEL)],
        out_shape=[jax.ShapeDtypeStruct((T, D_MODEL), F32)] * 2,
        compiler_params=_params("arbitrary"),
        name="merge",
    )(y, bonus, g, ya, z_g, x2, modr, modr, modr, lnx_g.reshape(1, -1), lnx_b.reshape(1, -1),
      norm2_g.reshape(1, -1), wbr, wba, wo, bd)


def _top16(s, n_rows):
    lanes = s.shape[1]
    row_iota = lax.broadcasted_iota(jnp.int32, s.shape, 0)
    out_iota = lax.broadcasted_iota(jnp.int32, (PEER_TOPK, lanes), 0)

    def it(r, carry):
        s, vals, idxs = carry
        m = jnp.max(s, axis=0, keepdims=True)
        am = jnp.min(jnp.where(s == m, row_iota, n_rows), axis=0, keepdims=True)
        vals = jnp.where(out_iota == r, m, vals)
        idxs = jnp.where(out_iota == r, am, idxs)
        s = jnp.where(row_iota == am, -jnp.inf, s)
        return s, vals, idxs

    init = (s, jnp.zeros((PEER_TOPK, lanes), F32), jnp.zeros((PEER_TOPK, lanes), jnp.int32))
    _, vals, idxs = lax.fori_loop(0, PEER_TOPK, it, init)
    return vals, idxs


def _split_bf16(x):
    hi = x.astype(BF16)
    return hi, (x - hi.astype(F32)).astype(BF16)


def _dot3(a, b, dims):
    dot = lambda p, q: lax.dot_general(p, q, (dims, ((), ())), preferred_element_type=F32)
    return dot(a[0], b[0]) + (dot(a[0], b[1]) + dot(a[1], b[0]))


def _topk_kernel(n2_ref, wqt_ref, keys_ref, idx_ref, gate_ref, qt_ref):
    tt = n2_ref.shape[0]
    K = PEER_TOPK
    nt = ((1,), (1,))
    nn = ((1,), (0,))
    qt_ref[...] = _dot3((wqt_ref[0], wqt_ref[1]), _split_bf16(n2_ref[...]), nt)
    k1 = (keys_ref[0], keys_ref[1])
    k2 = (keys_ref[2], keys_ref[3])

    def head(h, carry):
        base = pl.multiple_of(h * PEER_QUERY_DIM, PEER_QUERY_DIM)
        for c in range(tt // LANES):
            cols = slice(c * LANES, (c + 1) * LANES)
            q1 = _split_bf16(qt_ref[pl.ds(base, PEER_HALF), cols])
            q2 = _split_bf16(qt_ref[pl.ds(base + PEER_HALF, PEER_HALF), cols])
            v1, i1 = _top16(_dot3(k1, q1, nn), PEER_N_KEYS)
            v2, i2 = _top16(_dot3(k2, q2, nn), PEER_N_KEYS)
            cand = jnp.concatenate([v1[a:a + 1, :] + v2 for a in range(K)], axis=0)
            top, sel = _top16(cand, K * K)
            sa = lax.shift_right_logical(sel, 4)
            sb = sel & (K - 1)
            e1 = jnp.zeros((K, LANES), jnp.int32)
            e2 = jnp.zeros((K, LANES), jnp.int32)
            for a in range(K):
                e1 = jnp.where(sa == a, i1[a:a + 1, :], e1)
                e2 = jnp.where(sb == a, i2[a:a + 1, :], e2)
            ex = jnp.exp(top - top[0:1, :])
            rows = pl.ds(pl.multiple_of(h * K, K), K)
            idx_ref[rows, cols] = e1 * PEER_N_KEYS + e2
            gate_ref[rows, cols] = ex / jnp.sum(ex, axis=0, keepdims=True)
        return carry

    lax.fori_loop(0, PEER_HEADS, head, 0)


def _topk(n2, peer_wq, peer_k1, peer_k2):
    T = n2.shape[0]
    tt = TOPK_TOKENS
    wqt = jnp.stack(_split_bf16(peer_wq.T))
    keys = jnp.stack(_split_bf16(peer_k1) + _split_bf16(peer_k2))
    full = lambda a: pl.BlockSpec(a.shape, lambda i: (0, 0, 0))
    out = pl.BlockSpec((PEER_HK, tt), lambda i: (0, i))
    return pl.pallas_call(
        _topk_kernel,
        grid=(T // tt,),
        in_specs=[pl.BlockSpec((tt, D_MODEL), lambda i: (i, 0)), full(wqt), full(keys)],
        out_specs=[out, out],
        out_shape=[jax.ShapeDtypeStruct((PEER_HK, T), jnp.int32), jax.ShapeDtypeStruct((PEER_HK, T), F32)],
        scratch_shapes=[pltpu.VMEM((PEER_HEADS * PEER_QUERY_DIM, tt), F32)],
        compiler_params=_params("arbitrary"),
        name="topk",
    )(n2, wqt, keys)


def _peer_kernel(idx_ref, idx_next_ref, gate_ref, n2_ref, h_ref, gt2_ref, uv_ref, o_ref, buf, sem):
    tt = n2_ref.shape[0]
    hk = PEER_HK
    i = pl.program_id(0)
    n_steps = pl.num_programs(0)
    slot = i % 2

    def issue(src_idx_ref, dst_slot):
        def tok(t, carry):
            base = pl.multiple_of(t * hk, hk)
            for e in range(hk):
                row = src_idx_ref[base + e]
                pltpu.make_async_copy(uv_ref.at[pl.ds(row, 1), :], buf.at[dst_slot, t, pl.ds(e, 1), :],
                                      sem.at[dst_slot]).start()
            return carry
        lax.fori_loop(0, tt, tok, 0)

    @pl.when(i == 0)
    def _():
        issue(idx_ref, 0)

    @pl.when(i + 1 < n_steps)
    def _():
        issue(idx_next_ref, 1 - slot)

    def wait_tok(t, carry):
        pltpu.make_async_copy(uv_ref.at[pl.ds(0, hk), :], buf.at[slot, t], sem.at[slot]).wait()
        return carry
    lax.fori_loop(0, tt, wait_tok, 0)

    lane = lax.broadcasted_iota(jnp.int32, (hk, LANES), 1)
    lane0 = (i % (LANES // tt)) * tt
    gates = gate_ref[...]
    gt2 = gt2_ref[0]

    def tok(t, carry):
        u = n2_ref[pl.ds(t, 1), :]
        act = jnp.sum(buf[slot, t, :, 0:D_MODEL] * u, axis=-1, keepdims=True)
        act = 0.5 * act * (1.0 + lax.erf(act * (2.0 ** -0.5)))
        gcol = jnp.sum(jnp.where(lane == lane0 + t, gates, 0.0), axis=-1, keepdims=True)
        out = jnp.sum((gcol * act) * buf[slot, t, :, D_MODEL:2 * D_MODEL], axis=0, keepdims=True)
        o_ref[pl.ds(t, 1), :] = h_ref[pl.ds(t, 1), :] + gt2 * out
        return carry

    lax.fori_loop(0, tt, tok, 0)


def _peer(idx_flat, gate_t, n2, h, modr, uv, S):
    T = n2.shape[0]
    tt = PEER_TOKENS
    n_steps = T // tt
    tile = pl.BlockSpec((tt, D_MODEL), lambda i: (i, 0))
    return pl.pallas_call(
        _peer_kernel,
        grid=(n_steps,),
        in_specs=[pl.BlockSpec((tt * PEER_HK,), lambda i: (i,), memory_space=pltpu.SMEM),
                  pl.BlockSpec((tt * PEER_HK,), lambda i: (jnp.minimum(i + 1, n_steps - 1),), memory_space=pltpu.SMEM),
                  pl.BlockSpec((PEER_HK, LANES), lambda i: (0, i // (LANES // tt))),
                  tile, tile, _mod_spec(5, S // tt),
                  pl.BlockSpec(memory_space=pl.ANY)],
        out_specs=tile,
        out_shape=jax.ShapeDtypeStruct((T, D_MODEL), F32),
        scratch_shapes=[pltpu.VMEM((2, tt, PEER_HK, 2 * D_MODEL), F32), pltpu.SemaphoreType.DMA((2,))],
        compiler_params=_params("arbitrary"),
        name="peer",
    )(idx_flat, idx_flat, gate_t, n2, h, modr, uv)


def _block_diag_ones():
    i = np.arange(LANES)
    return jnp.asarray((i[:, None] // HEAD_DIM) == (i[None, :] // HEAD_DIM), dtype=BF16)


def kernel(x, c, w_ada, b_ada, norm1_g, w_in, rwkv_mu, w0, w2, a0, a2, g2, k_k, k_a, r_k, lnx_g, lnx_b, q_norm_g, k_norm_g, w_br_rwkv, w_br_attn, w_out, norm2_g, peer_wq, peer_k1, peer_k2, peer_u, peer_v):
    B, S, D = x.shape
    assert D == D_MODEL and w_ada.shape[0] == 1, "one layer of width D_MODEL"
    assert S % (ATTN_BLK * ATTN_GROUPS[-1][1]) == 0 and S % PREP_ROWS == 0
    T = B * S
    H, N = RWKV_HEADS, HEAD_DIM
    x2 = x.reshape(T, D)
    bd = _block_diag_ones()

    mod = _ada(c, w_ada[0], b_ada[0])
    modr = mod.reshape(B * N_MOD, 1, D)

    z_r, z_qkv, z_g = _inproj(x2, modr, norm1_g[0], w_in[0].astype(BF16), S)

    zpad = jnp.zeros((DECAY_LORA, RWKV_WIDTH), F32)
    w2p = jnp.concatenate([w2[0], zpad], axis=0).astype(BF16)
    a2p = jnp.concatenate([zpad, a2[0]], axis=0).astype(BF16)
    r, w, k, v, a, b, g, bonus = _prep(z_r, B, S, rwkv_mu[0], w0[0], a0[0], k_k[0], k_a[0], r_k[0],
                                       w2p, a2p, g2[0].astype(BF16), bd)

    to_lanes = lambda t: t.reshape(B, S, H, N).transpose(1, 3, 0, 2).reshape(S, N, B * H)
    y = _wkv(*(to_lanes(t) for t in (r, w, k, v, a, b)))
    y = y.reshape(S, N, B, H).transpose(2, 0, 3, 1).reshape(T, RWKV_WIDTH)

    ya = _attn(z_qkv, B, S, q_norm_g[0], k_norm_g[0])

    h, n2 = _merge(y, bonus, g, ya, z_g, x2, modr, lnx_g[0], lnx_b[0], norm2_g[0],
                   w_br_rwkv[0].astype(BF16), w_br_attn[0].astype(BF16), w_out[0].astype(BF16), bd, S)

    idx_t, gate_t = _topk(n2, peer_wq[0], peer_k1[0], peer_k2[0])
    uv = jnp.concatenate([peer_u[0], peer_v[0]], axis=1)
    out = _peer(idx_t.T.reshape(-1), gate_t, n2, h, modr, uv, S)
    return out.reshape(B, S, D)
```
